```python
import math
import jax
import jax.numpy as jnp
from jax import lax
import numpy as np

D_MODEL = 2048
BATCH = 4
SEQ = 2048
DEPTH = 1
DEC_BATCH = 8
DEC_SEQ = 64
PAST_LEN = 1024

CHUNK = 64
EPS = 1e-6
NEG_INF = -1e30
Q_BLOCK = 128
DA_HEADS = 8
DA_QK_DIM = 64
DA_V_DIM = 2 * DA_QK_DIM
DA_WIDTH = DA_HEADS * DA_V_DIM
ROPE_DIM = DA_QK_DIM // 4
ROPE_THETA = 500000.0
CM_GROUPS = 4
CM_GROUP_DIM = 128
CM_WIDTH = CM_GROUPS * CM_GROUP_DIM
CM_LEN = 128
MEM_TOKENS = 256
MEM_HEADS = 4
MEM_HEAD_DIM = 128
MEM_WIDTH = MEM_HEADS * MEM_HEAD_DIM
N_BRANCH = 3
MIX_WIDTH = DA_WIDTH + CM_WIDTH + MEM_WIDTH
IN_SPLITS = (DA_WIDTH, 2 * DA_WIDTH, 3 * DA_WIDTH, 3 * DA_WIDTH + CM_WIDTH, 3 * DA_WIDTH + 2 * CM_WIDTH, 3 * DA_WIDTH + 2 * CM_WIDTH + MEM_WIDTH)
IN_WIDTH = 3 * DA_WIDTH + 2 * CM_WIDTH + MEM_WIDTH + N_BRANCH * D_MODEL
PEER_HEADS = 8
PEER_NKEYS = 128
PEER_EXPERTS = PEER_NKEYS * PEER_NKEYS
PEER_QDIM = 256
PEER_HALF = PEER_QDIM // 2
PEER_TOPK = 16
PEER_TOKEN_BLOCK = 128

kernel_name = 'streaming_hybrid_diffattn_gmlp_mem_peer'


def rms_norm(x, g):
    xf = x.astype(jnp.float32)
    y = xf * lax.rsqrt(jnp.mean(xf * xf, axis=-1, keepdims=True) + EPS)
    return (y * g.astype(jnp.float32)).astype(x.dtype)


def layer_norm(x, g, b):
    xf = x.astype(jnp.float32)
    mu = jnp.mean(xf, axis=-1, keepdims=True)
    xc = xf - mu
    var = jnp.mean(xc * xc, axis=-1, keepdims=True)
    return (xc * lax.rsqrt(var + EPS) * g.astype(jnp.float32) + b.astype(jnp.float32)).astype(x.dtype)


def rope_partial(x, pos):
    half = ROPE_DIM // 2
    inv = ROPE_THETA ** (-jnp.arange(half, dtype=jnp.float32) / half)
    ang = pos.astype(jnp.float32)[:, None] * inv[None, :]
    cos = jnp.cos(ang)[:, None, None, :]
    sin = jnp.sin(ang)[:, None, None, :]
    xr = x[..., :ROPE_DIM].astype(jnp.float32)
    x1, x2 = xr[..., :half], xr[..., half:]
    rot = jnp.concatenate([x1 * cos - x2 * sin, x2 * cos + x1 * sin], axis=-1)
    return jnp.concatenate([rot.astype(x.dtype), x[..., ROPE_DIM:]], axis=-1)


def diff_attend(q, k, v, q_pos, k_pos, lam):
    s = jnp.einsum('bqhmd,bkhmd->bhmqk', q, k).astype(jnp.float32) * (DA_QK_DIM ** -0.5)
    mask = (k_pos[None, :] // CHUNK) <= (q_pos[:, None] // CHUNK)
    s = jnp.where(mask, s, NEG_INF)
    p = jax.nn.softmax(s, axis=-1)
    a = p[:, :, 0] - lam * p[:, :, 1]
    return jnp.einsum('bhqk,bkhd->bqhd', a.astype(v.dtype), v)


def diff_attention_blocked(q, k, v, pos, lam):
    B, T = q.shape[0], q.shape[1]
    nb = T // Q_BLOCK
    qb = jnp.moveaxis(q.reshape(B, nb, Q_BLOCK, DA_HEADS, 2, DA_QK_DIM), 1, 0)
    pb = pos.reshape(nb, Q_BLOCK)
    ob = lax.map(lambda a: diff_attend(a[0], k, v, a[1], pos, lam), (qb, pb))
    return jnp.moveaxis(ob, 0, 1).reshape(B, T, DA_HEADS, DA_V_DIM)


def chunk_mlp(u, v, w_s, b_s):
    B, T, _ = u.shape
    L = min(T, CM_LEN)
    nc = T // L
    ws = jnp.tril(w_s[:, :L, :L])
    vb = v.reshape(B, nc, L, CM_GROUPS, CM_GROUP_DIM)
    s = jnp.einsum('gts,bnsgc->bntgc', ws, vb) + jnp.transpose(b_s[:, :L])[None, None, :, :, None]
    return u * s.reshape(B, T, CM_WIDTH)


def mem_kv(mem, g, w, kn_g):
    B, M, _ = mem.shape
    kv = rms_norm(mem, g) @ w
    k, v = jnp.split(kv, 2, axis=-1)
    k = rms_norm(k.reshape(B, M, MEM_HEADS, MEM_HEAD_DIM), kn_g)
    return k, v.reshape(B, M, MEM_HEADS, MEM_HEAD_DIM)


def mem_attend(q, mem_k, mem_v):
    s = jnp.einsum('bqhd,bkhd->bhqk', q, mem_k).astype(jnp.float32) * (MEM_HEAD_DIM ** -0.5)
    p = jax.nn.softmax(s, axis=-1).astype(mem_v.dtype)
    return jnp.einsum('bhqk,bkhd->bqhd', p, mem_v)


def _peer_block(xb, w_query, sub_keys, expert_u, expert_v):
    tb = xb.shape[0]
    q = (xb @ w_query).reshape(tb, PEER_HEADS, 2, PEER_HALF)
    s = jnp.einsum('thpc,hpkc->thpk', q, sub_keys).astype(jnp.float32)
    s1, i1 = lax.top_k(s[:, :, 0], PEER_TOPK)
    s2, i2 = lax.top_k(s[:, :, 1], PEER_TOPK)
    n_cand = PEER_TOPK * PEER_TOPK
    cand_s = (s1[..., :, None] + s2[..., None, :]).reshape(tb, PEER_HEADS, n_cand)
    cand_i = (i1[..., :, None] * PEER_NKEYS + i2[..., None, :]).reshape(tb, PEER_HEADS, n_cand)
    top_s, top_j = lax.top_k(cand_s, PEER_TOPK)
    eidx = jnp.take_along_axis(cand_i, top_j, axis=-1)
    g = jax.nn.softmax(top_s, axis=-1).astype(xb.dtype)
    a = jax.nn.gelu(jnp.einsum('td,thkd->thk', xb, expert_u[eidx]), approximate=False)
    return jnp.einsum('thk,thkd->td', g * a, expert_v[eidx])


def peer(x, w_query, sub_keys, expert_u, expert_v):
    B, T, D = x.shape
    n = B * T
    pad = (-n) % PEER_TOKEN_BLOCK
    xf = jnp.pad(x.reshape(n, D), ((0, pad), (0, 0)))
    out = lax.map(lambda xb: _peer_block(xb, w_query, sub_keys, expert_u, expert_v), xf.reshape(-1, PEER_TOKEN_BLOCK, D))
    return out.reshape(-1, D)[:n].reshape(B, T, D)


def setup_inputs(seed: int = 0) -> dict:
    key = jax.random.key(seed)
    ks = jax.random.split(key, 32)
    f32 = jnp.float32

    def nrm(k, shape, scale):
        return jax.random.normal(k, shape, f32) * scale

    def gain(k, shape):
        return 1.0 + 0.01 * jax.random.normal(k, shape, f32)

    L = DEPTH
    return {
        'x_prompt': nrm(ks[0], (BATCH, SEQ, D_MODEL), 1.0),
        'x_sample': nrm(ks[1], (DEC_BATCH, DEC_SEQ, D_MODEL), 1.0),
        'mem_prompt': nrm(ks[2], (BATCH, MEM_TOKENS, D_MODEL), 1.0),
        'cache_da_k': nrm(ks[3], (L, DEC_BATCH, PAST_LEN, DA_HEADS, 2 * DA_QK_DIM), 1.0),
        'cache_da_v': nrm(ks[4], (L, DEC_BATCH, PAST_LEN, DA_HEADS, DA_V_DIM), 1.0),
        'cache_mem_k': nrm(ks[5], (L, DEC_BATCH, MEM_TOKENS, MEM_HEADS, MEM_HEAD_DIM), 1.0),
        'cache_mem_v': nrm(ks[6], (L, DEC_BATCH, MEM_TOKENS, MEM_HEADS, MEM_HEAD_DIM), 1.0),
        'norm_mix_g': gain(ks[7], (L, D_MODEL)),
        'w_in': nrm(ks[8], (L, D_MODEL, IN_WIDTH), D_MODEL ** -0.5),
        'b_gate': nrm(ks[9], (L, N_BRANCH * D_MODEL), 0.01),
        'da_qn_g': gain(ks[10], (L, DA_QK_DIM)),
        'da_kn_g': gain(ks[11], (L, DA_QK_DIM)),
        'da_lambda_q1': nrm(ks[12], (L, DA_QK_DIM), 0.1),
        'da_lambda_k1': nrm(ks[13], (L, DA_QK_DIM), 0.1),
        'da_lambda_q2': nrm(ks[14], (L, DA_QK_DIM), 0.1),
        'da_lambda_k2': nrm(ks[15], (L, DA_QK_DIM), 0.1),
        'da_out_g': gain(ks[16], (L, DA_V_DIM)),
        'cm_ln_g': gain(ks[17], (L, CM_WIDTH)),
        'cm_ln_b': nrm(ks[18], (L, CM_WIDTH), 0.01),
        'cm_ws': nrm(ks[19], (L, CM_GROUPS, CM_LEN, CM_LEN), CM_LEN ** -0.5),
        'cm_bs': gain(ks[20], (L, CM_GROUPS, CM_LEN)),
        'norm_mem_g': gain(ks[21], (L, D_MODEL)),
        'w_mem_kv': nrm(ks[22], (L, D_MODEL, 2 * MEM_WIDTH), D_MODEL ** -0.5),
        'mem_qn_g': gain(ks[23], (L, MEM_HEAD_DIM)),
        'mem_kn_g': gain(ks[24], (L, MEM_HEAD_DIM)),
        'w_branch': nrm(ks[25], (L, MIX_WIDTH, D_MODEL), MIX_WIDTH ** -0.5),
        'w_out': nrm(ks[26], (L, D_MODEL, D_MODEL), D_MODEL ** -0.5),
        'norm_ffn_g': gain(ks[27], (L, D_MODEL)),
        'peer_w_query': nrm(ks[28], (L, D_MODEL, PEER_HEADS * PEER_QDIM), D_MODEL ** -0.5),
        'peer_sub_keys': nrm(ks[29], (L, PEER_HEADS, 2, PEER_NKEYS, PEER_HALF), PEER_HALF ** -0.5),
        'peer_u': nrm(ks[30], (L, PEER_EXPERTS, D_MODEL), D_MODEL ** -0.5),
        'peer_v': nrm(ks[31], (L, PEER_EXPERTS, D_MODEL), (PEER_HEADS * PEER_TOPK) ** -0.5),
    }


def reference(x_prompt, x_sample, mem_prompt, cache_da_k, cache_da_v, cache_mem_k, cache_mem_v,
              norm_mix_g, w_in, b_gate, da_qn_g, da_kn_g, da_lambda_q1, da_lambda_k1,
              da_lambda_q2, da_lambda_k2, da_out_g, cm_ln_g, cm_ln_b, cm_ws, cm_bs,
              norm_mem_g, w_mem_kv, mem_qn_g, mem_kn_g, w_branch, w_out, norm_ffn_g,
              peer_w_query, peer_sub_keys, peer_u, peer_v):
    f32 = jnp.float32

    def mix_layer(l, x, pos, past_k, past_v, mem_k, mem_v):
        B, T, _ = x.shape
        lam_init = 0.8 - 0.6 * math.exp(-0.3 * l)
        lam = (jnp.exp(jnp.sum(da_lambda_q1[l].astype(f32) * da_lambda_k1[l].astype(f32)))
               - jnp.exp(jnp.sum(da_lambda_q2[l].astype(f32) * da_lambda_k2[l].astype(f32))) + lam_init)
        h = rms_norm(x, norm_mix_g[l])
        p = h @ w_in[l]
        q_da, k_da, v_da, u_cm, v_cm, q_mem, g_logit = jnp.split(p, list(IN_SPLITS), axis=-1)
        q = rope_partial(rms_norm(q_da.reshape(B, T, DA_HEADS, 2, DA_QK_DIM), da_qn_g[l]), pos)
        k = rope_partial(rms_norm(k_da.reshape(B, T, DA_HEADS, 2, DA_QK_DIM), da_kn_g[l]), pos)
        v = v_da.reshape(B, T, DA_HEADS, DA_V_DIM)
        if past_k is None:
            o = diff_attention_blocked(q, k, v, pos, lam)
        else:
            P = past_k.shape[1]
            k_all = jnp.concatenate([past_k.reshape(B, P, DA_HEADS, 2, DA_QK_DIM), k], axis=1)
            v_all = jnp.concatenate([past_v, v], axis=1)
            o = diff_attend(q, k_all, v_all, pos, jnp.arange(P + T), lam)
        o_da = (rms_norm(o, da_out_g[l]) * (1.0 - lam_init)).reshape(B, T, DA_WIDTH)
        u = jax.nn.gelu(u_cm, approximate=False)
        vn = layer_norm(jax.nn.gelu(v_cm, approximate=False), cm_ln_g[l], cm_ln_b[l])
        o_cm = chunk_mlp(u, vn, cm_ws[l], cm_bs[l])
        qm = rms_norm(q_mem.reshape(B, T, MEM_HEADS, MEM_HEAD_DIM), mem_qn_g[l])
        o_mem = mem_attend(qm, mem_k, mem_v).reshape(B, T, MEM_WIDTH)
        wb = w_branch[l]
        gates = jax.nn.sigmoid(g_logit.reshape(B, T, N_BRANCH, D_MODEL) + b_gate[l].reshape(N_BRANCH, D_MODEL))
        merged = (gates[:, :, 0] * (o_da @ wb[:DA_WIDTH])
                  + gates[:, :, 1] * (o_cm @ wb[DA_WIDTH:DA_WIDTH + CM_WIDTH])
                  + gates[:, :, 2] * (o_mem @ wb[DA_WIDTH + CM_WIDTH:]))
        x = x + merged @ w_out[l]
        x = x + peer(rms_norm(x, norm_ffn_g[l]), peer_w_query[l], peer_sub_keys[l], peer_u[l], peer_v[l])
        return x, k.reshape(B, T, DA_HEADS, 2 * DA_QK_DIM), v, vn

    pos_p = jnp.arange(x_prompt.shape[1])
    pos_s = cache_da_k.shape[2] + jnp.arange(x_sample.shape[1])
    yp, ys = x_prompt, x_sample
    kp_l, vp_l, mkp_l, mvp_l, ks_l, vs_l, cvs_l = [], [], [], [], [], [], []
    for l in range(DEPTH):
        mk, mv = mem_kv(mem_prompt, norm_mem_g[l], w_mem_kv[l], mem_kn_g[l])
        yp, kp, vp, _ = mix_layer(l, yp, pos_p, None, None, mk, mv)
        ys, ks, vs, cvs = mix_layer(l, ys, pos_s, cache_da_k[l], cache_da_v[l], cache_mem_k[l], cache_mem_v[l])
        kp_l.append(kp)
        vp_l.append(vp)
        mkp_l.append(mk)
        mvp_l.append(mv)
        ks_l.append(ks)
        vs_l.append(vs)
        cvs_l.append(cvs)
    new_da_k_prompt = jnp.stack(kp_l)
    new_da_v_prompt = jnp.stack(vp_l)
    new_mem_k_prompt = jnp.stack(mkp_l)
    new_mem_v_prompt = jnp.stack(mvp_l)
    new_da_k_sample = jnp.stack(ks_l)
    new_da_v_sample = jnp.stack(vs_l)
    new_cm_v_sample = jnp.stack(cvs_l)
    return (yp, ys, new_da_k_prompt, new_da_v_prompt, new_mem_k_prompt, new_mem_v_prompt, new_da_k_sample, new_da_v_sample, new_cm_v_sample)
```

```python
import functools
import math

import jax
import jax.numpy as jnp
from jax import lax
from jax.experimental import pallas as pl
from jax.experimental.pallas import tpu as pltpu

F32 = jnp.float32
BF16 = jnp.bfloat16

D_MODEL = 2048
CHUNK = 64
EPS = 1e-6
NEG_INF = -1e30
DA_HEADS = 8
DA_QK_DIM = 64
DA_V_DIM = 128
DA_WIDTH = DA_HEADS * DA_V_DIM
ROPE_DIM = DA_QK_DIM // 4
ROPE_THETA = 500000.0
CM_GROUPS = 4
CM_GROUP_DIM = 128
CM_WIDTH = CM_GROUPS * CM_GROUP_DIM
CM_LEN = 128
MEM_TOKENS = 256
MEM_HEADS = 4
MEM_HEAD_DIM = 128
MEM_WIDTH = MEM_HEADS * MEM_HEAD_DIM
N_BRANCH = 3
PEER_HEADS = 8
PEER_NKEYS = 128
PEER_EXPERTS = PEER_NKEYS * PEER_NKEYS
PEER_HALF = 128
PEER_TOPK = 16

LANES = 128
COL_Q, COL_K, COL_V = 0, DA_WIDTH, 2 * DA_WIDTH
COL_RAW = 3 * DA_WIDTH
RAW_WIDTH = 2 * CM_WIDTH + MEM_WIDTH
COL_GATE = COL_RAW + RAW_WIDTH

VMEM_LIMIT = 56 * 1024 * 1024


def _params(*sem):
    return pltpu.CompilerParams(dimension_semantics=sem, vmem_limit_bytes=VMEM_LIMIT)


def _nt(a, b):
    return lax.dot_general(a, b, (((1,), (1,)), ((), ())), preferred_element_type=F32)


SQRT_HALF = 0.7071067811865476


def _gelu(x):
    return 0.5 * x * (1.0 + lax.erf(x * SQRT_HALF))


def _rms_lanes(x, g):
    ms = jnp.mean(x * x, axis=-1, keepdims=True)
    return x * lax.rsqrt(ms + EPS) * g


def _rmsnorm_body(x_ref, g_ref, o_ref):
    o_ref[...] = _rms_lanes(x_ref[...], g_ref[...]).astype(o_ref.dtype)


def _rmsnorm(x, g, tm):
    t, d = x.shape
    return pl.pallas_call(
        _rmsnorm_body,
        out_shape=jax.ShapeDtypeStruct((t, d), BF16),
        grid=(t // tm,),
        in_specs=[pl.BlockSpec((tm, d), lambda i: (i, 0)), pl.BlockSpec((1, d), lambda i: (0, 0))],
        out_specs=pl.BlockSpec((tm, d), lambda i: (i, 0)),
        compiler_params=_params("parallel"),
        name="rmsnorm",
    )(x, g.reshape(1, d))


def _proj_body(h_ref, w_ref, *rest, mode, tn, scale):
    acc = jnp.dot(h_ref[...], w_ref[...], preferred_element_type=F32)
    if mode == "plain":
        (o_ref,) = rest
        o_ref[...] = acc
    elif mode == "gate":
        b_ref, o_ref = rest
        o_ref[...] = jax.nn.sigmoid(acc + b_ref[...]).astype(o_ref.dtype)
    else:
        g_ref, c_ref, su_ref, sd_ref, o_ref = rest
        r0 = lax.broadcasted_iota(jnp.int32, (LANES, LANES), 0) // DA_QK_DIM
        r1 = lax.broadcasted_iota(jnp.int32, (LANES, LANES), 1) // DA_QK_DIM
        seg = jnp.where(r0 == r1, 1.0, 0.0).astype(BF16)
        for b in range(tn // LANES):
            x = acc[:, b * LANES:(b + 1) * LANES]
            x2 = x * x
            hi = x2.astype(BF16)
            lo = (x2 - hi.astype(F32)).astype(BF16)
            ss = (jnp.dot(hi, seg, preferred_element_type=F32)
                  + jnp.dot(lo, seg, preferred_element_type=F32))
            y = x * lax.rsqrt(ss * (1.0 / DA_QK_DIM) + EPS) * g_ref[...]
            r = (y * c_ref[...]
                 + pltpu.roll(y, LANES - ROPE_DIM // 2, 1) * su_ref[...]
                 + pltpu.roll(y, ROPE_DIM // 2, 1) * sd_ref[...])
            o_ref[:, b * LANES:(b + 1) * LANES] = (r * scale).astype(o_ref.dtype)


def _proj(h, w, col0, width, mode, out_dtype, tm, tn=512, extras=(), scale=1.0):
    t, d = h.shape
    c0 = col0 // tn
    in_specs = [
        pl.BlockSpec((tm, d), lambda i, j: (i, 0)),
        pl.BlockSpec((d, tn), lambda i, j: (0, c0 + j)),
    ]
    if mode == "gate":
        in_specs.append(pl.BlockSpec((1, tn), lambda i, j: (0, j)))
    elif mode == "rope":
        in_specs.append(pl.BlockSpec((1, LANES), lambda i, j: (0, 0)))
        in_specs += [pl.BlockSpec((tm, LANES), lambda i, j: (i, 0))] * 3
    return pl.pallas_call(
        functools.partial(_proj_body, mode=mode, tn=tn, scale=scale),
        out_shape=jax.ShapeDtypeStruct((t, width), out_dtype),
        grid=(t // tm, width // tn),
        in_specs=in_specs,
        out_specs=pl.BlockSpec((tm, tn), lambda i, j: (i, j)),
        compiler_params=_params("parallel", "parallel"),
        name="proj_" + mode,
    )(h, w, *extras)


def _memkv_body(m_ref, g_ref, w_ref, kg_ref, k_ref, v_ref):
    h = _rms_lanes(m_ref[0], g_ref[...]).astype(BF16)
    kv = jnp.dot(h, w_ref[...], preferred_element_type=F32)
    for hd in range(MEM_HEADS):
        blk = kv[:, hd * MEM_HEAD_DIM:(hd + 1) * MEM_HEAD_DIM]
        k_ref[0, :, hd * MEM_HEAD_DIM:(hd + 1) * MEM_HEAD_DIM] = _rms_lanes(blk, kg_ref[...])
    v_ref[0] = kv[:, MEM_WIDTH:]


def _mem_kv(mem, g, w_bf, kn_g):
    b, m, d = mem.shape
    out = jax.ShapeDtypeStruct((b, m, MEM_WIDTH), F32)
    return pl.pallas_call(
        _memkv_body,
        out_shape=(out, out),
        grid=(b,),
        in_specs=[
            pl.BlockSpec((1, m, d), lambda i: (i, 0, 0)),
            pl.BlockSpec((1, d), lambda i: (0, 0)),
            pl.BlockSpec((d, 2 * MEM_WIDTH), lambda i: (0, 0)),
            pl.BlockSpec((1, MEM_HEAD_DIM), lambda i: (0, 0)),
        ],
        out_specs=(pl.BlockSpec((1, m, MEM_WIDTH), lambda i: (i, 0, 0)),) * 2,
        compiler_params=_params("parallel"),
        name="mem_kv",
    )(mem, g.reshape(1, d), w_bf, kn_g.reshape(1, MEM_HEAD_DIM))


def _lam(lp_ref, lam_init):
    a = jnp.sum(lp_ref[0:1, :] * lp_ref[1:2, :], axis=-1, keepdims=True)
    b = jnp.sum(lp_ref[2:3, :] * lp_ref[3:4, :], axis=-1, keepdims=True)
    return jnp.exp(a) - jnp.exp(b) + lam_init


def _split_q(q):
    lane = lax.broadcasted_iota(jnp.int32, q.shape, 1)
    zero = jnp.zeros_like(q)
    return jnp.where(lane < DA_QK_DIM, q, zero), jnp.where(lane >= DA_QK_DIM, q, zero)


def _chunk_mask(q_pos0, tq, k_pos0, tk):
    qc = (q_pos0 + lax.broadcasted_iota(jnp.int32, (tq, 1), 0)) // CHUNK
    kc = (k_pos0 + lax.broadcasted_iota(jnp.int32, (1, tk), 1)) // CHUNK
    return kc <= qc


def _da_out(o, og_ref, lam_init):
    return (_rms_lanes(o, og_ref[...]) * (1.0 - lam_init)).astype(BF16)


def _da_prompt_body(q_ref, k_ref, v_ref, lp_ref, og_ref, o_ref, *, t, tq, lam_init):
    lam = _lam(lp_ref, lam_init)
    kb = k_ref[0].astype(BF16)
    vb = v_ref[0].astype(BF16)
    for qi in range(t // tq):
        klen = (qi + 1) * tq
        q0, q1 = _split_q(q_ref[0, qi * tq:(qi + 1) * tq, :])
        kk = kb[:klen]
        mask = _chunk_mask(qi * tq, tq, 0, klen)
        s0 = jnp.where(mask, _nt(q0, kk), NEG_INF)
        s1 = jnp.where(mask, _nt(q1, kk), NEG_INF)
        e0 = jnp.exp(s0 - jnp.max(s0, axis=-1, keepdims=True))
        e1 = jnp.exp(s1 - jnp.max(s1, axis=-1, keepdims=True))
        i0 = 1.0 / jnp.sum(e0, axis=-1, keepdims=True)
        i1 = lam / jnp.sum(e1, axis=-1, keepdims=True)
        a = (e0 * i0 - e1 * i1).astype(BF16)
        o = jnp.dot(a, vb[:klen], preferred_element_type=F32)
        o_ref[0, qi * tq:(qi + 1) * tq, :] = _da_out(o, og_ref, lam_init)


def _da_prompt(q, k, v, lam_p, out_g, lam_init, tq=256):
    b, t, _ = q.shape
    blk = lambda: pl.BlockSpec((1, t, LANES), lambda i, h: (i, 0, h))
    return pl.pallas_call(
        functools.partial(_da_prompt_body, t=t, tq=tq, lam_init=lam_init),
        out_shape=jax.ShapeDtypeStruct((b, t, DA_WIDTH), BF16),
        grid=(b, DA_HEADS),
        in_specs=[blk(), blk(), blk(),
                  pl.BlockSpec((4, DA_QK_DIM), lambda i, h: (0, 0)),
                  pl.BlockSpec((1, DA_V_DIM), lambda i, h: (0, 0))],
        out_specs=blk(),
        compiler_params=_params("parallel", "parallel"),
        name="da_prompt",
    )(q, k, v, lam_p, out_g.reshape(1, DA_V_DIM))


def _da_sample_body(q_ref, k_ref, v_ref, pk_ref, pv_ref, lp_ref, og_ref, o_ref, *, t, p, lam_init):
    lam = _lam(lp_ref, lam_init)
    q0, q1 = _split_q(q_ref[0])
    kn = k_ref[0].astype(BF16)
    kp = pk_ref[0].astype(BF16)
    mask_p = _chunk_mask(p, t, 0, p)
    mask_n = _chunk_mask(p, t, p, t)

    def parts(qh):
        sp = jnp.where(mask_p, _nt(qh, kp), NEG_INF)
        sn = jnp.where(mask_n, _nt(qh, kn), NEG_INF)
        m = jnp.maximum(jnp.max(sp, axis=-1, keepdims=True), jnp.max(sn, axis=-1, keepdims=True))
        ep = jnp.exp(sp - m)
        en = jnp.exp(sn - m)
        z = jnp.sum(ep, axis=-1, keepdims=True) + jnp.sum(en, axis=-1, keepdims=True)
        return ep, en, z

    ep0, en0, z0 = parts(q0)
    ep1, en1, z1 = parts(q1)
    i0 = 1.0 / z0
    i1 = lam / z1
    ap = (ep0 * i0 - ep1 * i1).astype(BF16)
    an = (en0 * i0 - en1 * i1).astype(BF16)
    o = (jnp.dot(ap, pv_ref[0].astype(BF16), preferred_element_type=F32)
         + jnp.dot(an, v_ref[0].astype(BF16), preferred_element_type=F32))
    o_ref[0] = _da_out(o, og_ref, lam_init)


def _da_sample(q, k, v, past_k, past_v, lam_p, out_g, lam_init):
    b, t, _ = q.shape
    p = past_k.shape[1]
    new = lambda: pl.BlockSpec((1, t, LANES), lambda i, h: (i, 0, h))
    past = lambda: pl.BlockSpec((1, p, LANES), lambda i, h: (i, 0, h))
    return pl.pallas_call(
        functools.partial(_da_sample_body, t=t, p=p, lam_init=lam_init),
        out_shape=jax.ShapeDtypeStruct((b, t, DA_WIDTH), BF16),
        grid=(b, DA_HEADS),
        in_specs=[new(), new(), new(), past(), past(),
                  pl.BlockSpec((4, DA_QK_DIM), lambda i, h: (0, 0)),
                  pl.BlockSpec((1, DA_V_DIM), lambda i, h: (0, 0))],
        out_specs=new(),
        compiler_params=_params("parallel", "parallel"),
        name="da_sample",
    )(q, k, v, past_k, past_v, lam_p, out_g.reshape(1, DA_V_DIM))


def _mem_attn_body(q_ref, mk_ref, mv_ref, g_ref, o_ref):
    for hd in range(MEM_HEADS):
        sl = slice(hd * MEM_HEAD_DIM, (hd + 1) * MEM_HEAD_DIM)
        q = _rms_lanes(q_ref[0, :, sl], g_ref[...]).astype(BF16)
        s = _nt(q, mk_ref[0, :, sl].astype(BF16)) * (MEM_HEAD_DIM ** -0.5)
        e = jnp.exp(s - jnp.max(s, axis=-1, keepdims=True))
        pr = (e * (1.0 / jnp.sum(e, axis=-1, keepdims=True))).astype(BF16)
        o = jnp.dot(pr, mv_ref[0, :, sl].astype(BF16), preferred_element_type=F32)
        o_ref[0, :, sl] = o.astype(o_ref.dtype)


def _mem_attn(raw, mem_k, mem_v, qn_g, tq):
    b, t, _ = raw.shape
    m = mem_k.shape[1]
    qblk = (2 * CM_WIDTH) // MEM_WIDTH
    return pl.pallas_call(
        _mem_attn_body,
        out_shape=jax.ShapeDtypeStruct((b, t, MEM_WIDTH), BF16),
        grid=(b, t // tq),
        in_specs=[
            pl.BlockSpec((1, tq, MEM_WIDTH), lambda i, j: (i, j, qblk)),
            pl.BlockSpec((1, m, MEM_WIDTH), lambda i, j: (i, 0, 0)),
            pl.BlockSpec((1, m, MEM_WIDTH), lambda i, j: (i, 0, 0)),
            pl.BlockSpec((1, MEM_HEAD_DIM), lambda i, j: (0, 0)),
        ],
        out_specs=pl.BlockSpec((1, tq, MEM_WIDTH), lambda i, j: (i, j, 0)),
        compiler_params=_params("parallel", "parallel"),
        name="mem_attn",
    )(raw, mem_k, mem_v, qn_g.reshape(1, MEM_HEAD_DIM))


def _gmlp_body(u_ref, v_ref, lg_ref, lb_ref, ws_ref, bs_ref, o_ref, vn_ref, *, ln):
    u = _gelu(u_ref[0])
    gv = _gelu(v_ref[0])
    mu = jnp.mean(gv, axis=-1, keepdims=True)
    xc = gv - mu
    var = jnp.mean(xc * xc, axis=-1, keepdims=True)
    vn = xc * lax.rsqrt(var + EPS) * lg_ref[...] + lb_ref[...]
    vn_ref[0] = vn
    row = lax.broadcasted_iota(jnp.int32, (ln, ln), 0)
    col = lax.broadcasted_iota(jnp.int32, (ln, ln), 1)
    for g in range(CM_GROUPS):
        sl = slice(g * CM_GROUP_DIM, (g + 1) * CM_GROUP_DIM)
        ws = jnp.where(col <= row, ws_ref[g], 0.0).astype(BF16)
        s = jnp.dot(ws, vn[:, sl].astype(BF16), preferred_element_type=F32) + bs_ref[:, g:g + 1]
        o_ref[0, :, sl] = (u[:, sl] * s).astype(o_ref.dtype)


def _gmlp(raw, ln_g, ln_b, ws, bs_t, ln):
    b, t, _ = raw.shape
    return pl.pallas_call(
        functools.partial(_gmlp_body, ln=ln),
        out_shape=(jax.ShapeDtypeStruct((b, t, CM_WIDTH), BF16),
                   jax.ShapeDtypeStruct((b, t, CM_WIDTH), F32)),
        grid=(b, t // ln),
        in_specs=[
            pl.BlockSpec((1, ln, CM_WIDTH), lambda i, j: (i, j, 0)),
            pl.BlockSpec((1, ln, CM_WIDTH), lambda i, j: (i, j, 1)),
            pl.BlockSpec((1, CM_WIDTH), lambda i, j: (0, 0)),
            pl.BlockSpec((1, CM_WIDTH), lambda i, j: (0, 0)),
            pl.BlockSpec((CM_GROUPS, ln, ln), lambda i, j: (0, 0, 0)),
            pl.BlockSpec((ln, CM_GROUPS), lambda i, j: (0, 0)),
        ],
        out_specs=(pl.BlockSpec((1, ln, CM_WIDTH), lambda i, j: (i, j, 0)),) * 2,
        compiler_params=_params("parallel", "parallel"),
        name="gmlp",
    )(raw, raw, ln_g.reshape(1, CM_WIDTH), ln_b.reshape(1, CM_WIDTH), ws, bs_t)


def _merge_body(x_ref, da_ref, cm_ref, mm_ref, gt_ref, wb_ref, wo_ref, o_ref):
    def branch(i, a_ref, r0, r1):
        pr = jnp.dot(a_ref[...], wb_ref[r0:r1, :], preferred_element_type=F32)
        return gt_ref[:, i * D_MODEL:(i + 1) * D_MODEL].astype(F32) * pr

    merged = (branch(0, da_ref, 0, DA_WIDTH)
              + branch(1, cm_ref, DA_WIDTH, DA_WIDTH + CM_WIDTH)
              + branch(2, mm_ref, DA_WIDTH + CM_WIDTH, DA_WIDTH + CM_WIDTH + MEM_WIDTH))
    o_ref[...] = x_ref[...] + jnp.dot(merged.astype(BF16), wo_ref[...], preferred_element_type=F32)


def _merge(x, o_da, o_cm, o_mem, gates, wb_bf, wo_bf, tm):
    t, d = x.shape
    row = lambda w: pl.BlockSpec((tm, w), lambda i: (i, 0))
    res = lambda s: pl.BlockSpec(s, lambda i: (0, 0), pipeline_mode=pl.Buffered(1))
    return pl.pallas_call(
        _merge_body,
        out_shape=jax.ShapeDtypeStruct((t, d), F32),
        grid=(t // tm,),
        in_specs=[row(d), row(DA_WIDTH), row(CM_WIDTH), row(MEM_WIDTH), row(N_BRANCH * d),
                  res(wb_bf.shape), res(wo_bf.shape)],
        out_specs=row(d),
        compiler_params=_params("parallel"),
        name="merge",
    )(x, o_da, o_cm, o_mem, gates, wb_bf, wo_bf)


def _top_desc(x, k):
    rows = []
    cur = x
    for _ in range(k):
        m = jnp.max(cur, axis=0, keepdims=True)
        rows.append(m)
        cur = jnp.where(cur == m, -jnp.inf, cur)
    return jnp.concatenate(rows, axis=0)


def _route_body(x_ref, g_ref, wq_ref, keys_ref, ht_ref, s1_ref, s2_ref, a_ref, b_ref, tau_ref):
    h2 = _rms_lanes(x_ref[...], g_ref[...])
    ht_ref[...] = h2.T.astype(BF16)
    q = jnp.dot(h2.astype(BF16), wq_ref[...], preferred_element_type=F32).astype(BF16)
    half = PEER_TOPK // 2
    for hd in range(PEER_HEADS):
        c0 = hd * 2 * PEER_HALF
        s1 = _nt(keys_ref[hd, 0], q[:, c0:c0 + PEER_HALF])
        s2 = _nt(keys_ref[hd, 1], q[:, c0 + PEER_HALF:c0 + 2 * PEER_HALF])
        v1 = _top_desc(s1, PEER_TOPK)
        v2 = _top_desc(s2, PEER_TOPK)
        cand = jnp.concatenate(
            [v1 + v2[0:1]]
            + [v1[0:half] + v2[l:l + 1] for l in range(1, half)]
            + [v1[0:1] + v2[half:]], axis=0)
        top = _top_desc(cand, PEER_TOPK)
        m = top[0:1]
        tau = top[PEER_TOPK - 1:PEER_TOPK]
        z = jnp.sum(jnp.where(cand >= tau, jnp.exp(cand - m), 0.0), axis=0, keepdims=True)
        s1_ref[hd] = s1
        s2_ref[hd] = s2
        a_ref[hd] = jnp.exp(s1 - v1[0:1]) * (1.0 / z)
        b_ref[hd] = jnp.exp(s2 - v2[0:1])
        tau_ref[hd] = tau


def _route(x1, g, wq_bf, keys_bf, tm):
    t, d = x1.shape
    tab = jax.ShapeDtypeStruct((PEER_HEADS, PEER_NKEYS, t), F32)
    tab_spec = pl.BlockSpec((PEER_HEADS, PEER_NKEYS, tm), lambda i: (0, 0, i))
    return pl.pallas_call(
        _route_body,
        out_shape=(jax.ShapeDtypeStruct((d, t), BF16), tab, tab, tab, tab,
                   jax.ShapeDtypeStruct((PEER_HEADS, 1, t), F32)),
        grid=(t // tm,),
        in_specs=[
            pl.BlockSpec((tm, d), lambda i: (i, 0)),
            pl.BlockSpec((1, d), lambda i: (0, 0)),
            pl.BlockSpec(wq_bf.shape, lambda i: (0, 0), pipeline_mode=pl.Buffered(1)),
            pl.BlockSpec(keys_bf.shape, lambda i: (0, 0, 0, 0)),
        ],
        out_specs=(pl.BlockSpec((d, tm), lambda i: (0, i)), tab_spec, tab_spec, tab_spec, tab_spec,
                   pl.BlockSpec((PEER_HEADS, 1, tm), lambda i: (0, 0, i))),
        compiler_params=_params("parallel"),
        name="peer_route",
    )(x1, g.reshape(1, d), wq_bf, keys_bf)


def _peer_body(x_ref, ht_ref, u_ref, v_ref, s1_ref, s2_ref, a_ref, b_ref, tau_ref, y_ref, wa_ref,
               *, tm, te):
    j = pl.program_id(1)
    act = jnp.dot(u_ref[...], ht_ref[...], preferred_element_type=F32)
    for cc in range(te // LANES):
        c = j * (te // LANES) + cc
        s1_rows = [s1_ref[hd, pl.ds(c, 1), :] for hd in range(PEER_HEADS)]
        a_rows = [a_ref[hd, pl.ds(c, 1), :] for hd in range(PEER_HEADS)]
        for tg in range(tm // LANES):
            tl = slice(tg * LANES, (tg + 1) * LANES)
            w = jnp.zeros((LANES, LANES), F32)
            for hd in range(PEER_HEADS):
                s = s1_rows[hd][:, tl] + s2_ref[hd, :, tl]
                wgt = a_rows[hd][:, tl] * b_ref[hd, :, tl]
                w = w + jnp.where(s >= tau_ref[hd, :, tl], wgt, 0.0)
            wa = w * _gelu(act[cc * LANES:(cc + 1) * LANES, tl])
            wa_ref[tl, cc * LANES:(cc + 1) * LANES] = wa.T.astype(BF16)
    upd = jnp.dot(wa_ref[...], v_ref[...], preferred_element_type=F32)

    @pl.when(j == 0)
    def _():
        y_ref[...] = x_ref[...] + upd

    @pl.when(j != 0)
    def _():
        y_ref[...] += upd


def _peer(x1, ht, u_bf, v_bf, s1, s2, a, b, tau, tm, te=512):
    t, d = x1.shape
    tab_spec = pl.BlockSpec((PEER_HEADS, PEER_NKEYS, tm), lambda i, j: (0, 0, i))
    return pl.pallas_call(
        functools.partial(_peer_body, tm=tm, te=te),
        out_shape=jax.ShapeDtypeStruct((t, d), F32),
        grid=(t // tm, PEER_EXPERTS // te),
        in_specs=[
            pl.BlockSpec((tm, d), lambda i, j: (i, 0)),
            pl.BlockSpec((d, tm), lambda i, j: (0, i)),
            pl.BlockSpec((te, d), lambda i, j: (j, 0)),
            pl.BlockSpec((te, d), lambda i, j: (j, 0)),
            tab_spec, tab_spec, tab_spec, tab_spec,
            pl.BlockSpec((PEER_HEADS, 1, tm), lambda i, j: (0, 0, i)),
        ],
        out_specs=pl.BlockSpec((tm, d), lambda i, j: (i, 0)),
        scratch_shapes=[pltpu.VMEM((tm, te), BF16)],
        compiler_params=_params("parallel", "arbitrary"),
        name="peer_experts",
    )(x1, ht, u_bf, v_bf, s1, s2, a, b, tau)


def _rope_tables(pos):
    half = ROPE_DIM // 2
    inv = ROPE_THETA ** (-jnp.arange(half, dtype=F32) / half)
    ang = pos.astype(F32)[:, None] * inv[None, :]
    cos, sin = jnp.cos(ang), jnp.sin(ang)
    n = pos.shape[0]
    one = jnp.ones((n, DA_QK_DIM - ROPE_DIM), F32)
    zero = jnp.zeros((n, DA_QK_DIM - ROPE_DIM), F32)
    zh = jnp.zeros((n, half), F32)
    c = jnp.concatenate([cos, cos, one], axis=1)
    su = jnp.concatenate([-sin, zh, zero], axis=1)
    sd = jnp.concatenate([zh, sin, zero], axis=1)
    return tuple(jnp.tile(a, (1, LANES // DA_QK_DIM)) for a in (c, su, sd))


def _mix_layer(l, x, pos, past_k, past_v, mem_k, mem_v, wts):
    bn, tn, d = x.shape
    t = bn * tn
    xf = x.reshape(t, d)
    lam_init = 0.8 - 0.6 * math.exp(-0.3 * l)
    tm = min(1024, t)

    tabs = tuple(jnp.tile(a, (bn, 1)) for a in _rope_tables(pos))
    h = _rmsnorm(xf, wts["norm_mix_g"], 512)
    w_in = wts["w_in"]
    gtile = lambda g: jnp.tile(g, LANES // DA_QK_DIM).reshape(1, LANES)
    q = _proj(h, w_in, COL_Q, DA_WIDTH, "rope", BF16, tm,
              extras=(gtile(wts["da_qn_g"]),) + tabs, scale=DA_QK_DIM ** -0.5)
    k = _proj(h, w_in, COL_K, DA_WIDTH, "rope", F32, tm, extras=(gtile(wts["da_kn_g"]),) + tabs)
    v = _proj(h, w_in, COL_V, DA_WIDTH, "plain", F32, tm)
    raw = _proj(h, w_in, COL_RAW, RAW_WIDTH, "plain", F32, tm)
    gates = _proj(h, w_in, COL_GATE, N_BRANCH * d, "gate", BF16, tm,
                  extras=(wts["b_gate"].reshape(1, N_BRANCH * d),))

    q3, k3, v3 = (a.reshape(bn, tn, DA_WIDTH) for a in (q, k, v))
    raw3 = raw.reshape(bn, tn, RAW_WIDTH)
    if past_k is None:
        o_da = _da_prompt(q3, k3, v3, wts["lam_p"], wts["da_out_g"], lam_init)
    else:
        o_da = _da_sample(q3, k3, v3, past_k, past_v, wts["lam_p"], wts["da_out_g"], lam_init)

    ln = min(tn, CM_LEN)
    o_cm, vn = _gmlp(raw3, wts["cm_ln_g"], wts["cm_ln_b"], wts["cm_ws"][:, :ln, :ln],
                     jnp.transpose(wts["cm_bs"][:, :ln]), ln)
    o_mem = _mem_attn(raw3, mem_k, mem_v, wts["mem_qn_g"], min(tn, 512))

    x1 = _merge(xf, o_da.reshape(t, DA_WIDTH), o_cm.reshape(t, CM_WIDTH), o_mem.reshape(t, MEM_WIDTH),
                gates, wts["w_branch"], wts["w_out"], 512)

    ht, s1, s2, a, b, tau = _route(x1, wts["norm_ffn_g"], wts["peer_w_query"], wts["peer_sub_keys"], 256)
    y = _peer(x1, ht, wts["peer_u"], wts["peer_v"], s1, s2, a, b, tau, 512)
    return y.reshape(bn, tn, d), k3, v3, vn


def kernel(x_prompt, x_sample, mem_prompt, cache_da_k, cache_da_v, cache_mem_k, cache_mem_v, norm_mix_g, w_in, b_gate, da_qn_g, da_kn_g, da_lambda_q1, da_lambda_k1, da_lambda_q2, da_lambda_k2, da_out_g, cm_ln_g, cm_ln_b, cm_ws, cm_bs, norm_mem_g, w_mem_kv, mem_qn_g, mem_kn_g, w_branch, w_out, norm_ffn_g, peer_w_query, peer_sub_keys, peer_u, peer_v):
    depth = w_in.shape[0]
    bp, tp, _ = x_prompt.shape
    bs, ts, _ = x_sample.shape
    past = cache_da_k.shape[2]
    pos_p = jnp.arange(tp)
    pos_s = past + jnp.arange(ts)
    yp, ys = x_prompt, x_sample
    outs = [[] for _ in range(7)]
    for l in range(depth):
        wts = dict(
            norm_mix_g=norm_mix_g[l], w_in=w_in[l].astype(BF16), b_gate=b_gate[l],
            da_qn_g=da_qn_g[l], da_kn_g=da_kn_g[l],
            lam_p=jnp.stack([da_lambda_q1[l], da_lambda_k1[l], da_lambda_q2[l], da_lambda_k2[l]]),
            da_out_g=da_out_g[l], cm_ln_g=cm_ln_g[l], cm_ln_b=cm_ln_b[l], cm_ws=cm_ws[l], cm_bs=cm_bs[l],
            mem_qn_g=mem_qn_g[l], w_branch=w_branch[l].astype(BF16), w_out=w_out[l].astype(BF16),
            norm_ffn_g=norm_ffn_g[l], peer_w_query=peer_w_query[l].astype(BF16),
            peer_sub_keys=peer_sub_keys[l].astype(BF16),
            peer_u=peer_u[l].astype(BF16), peer_v=peer_v[l].astype(BF16),
        )
        mk, mv = _mem_kv(mem_prompt, norm_mem_g[l], w_mem_kv[l].astype(BF16), mem_kn_g[l])
        yp, kp, vp, _ = _mix_layer(l, yp, pos_p, None, None, mk, mv, wts)
        ys, ks, vs, cvs = _mix_layer(
            l, ys, pos_s,
            cache_da_k[l].reshape(bs, past, DA_WIDTH), cache_da_v[l].reshape(bs, past, DA_WIDTH),
            cache_mem_k[l].reshape(bs, MEM_TOKENS, MEM_WIDTH), cache_mem_v[l].reshape(bs, MEM_TOKENS, MEM_WIDTH),
            wts)
        for lst, val in zip(outs, (
                kp.reshape(bp, tp, DA_HEADS, DA_V_DIM), vp.reshape(bp, tp, DA_HEADS, DA_V_DIM),
                mk.reshape(bp, MEM_TOKENS, MEM_HEADS, MEM_HEAD_DIM),
                mv.reshape(bp, MEM_TOKENS, MEM_HEADS, MEM_HEAD_DIM),
                ks.reshape(bs, ts, DA_HEADS, DA_V_DIM), vs.reshape(bs, ts, DA_HEADS, DA_V_DIM), cvs)):
            lst.append(val)
    return (yp, ys) + tuple(jnp.stack(o) for o in outs)
```
